```python
import math
import jax
import jax.numpy as jnp
from jax import lax
import numpy as np

D_MODEL = 1024
BATCH = 16
SEQ = 2048
DEPTH = 4

GRID_W = 64
CTX_LEN = 256
N_MIXERS = 3
N_HYENA = (DEPTH + 2) // 3
N_S5 = (DEPTH + 1) // 3
N_MLSTM = DEPTH // 3
EPS = 1e-6

HY_CONV = 3
HY_EMB = 33
HY_BANDS = (HY_EMB - 1) // 2
HY_ORDER = 64
HY_FAST_DECAY = 0.3
HY_SLOW_DECAY = 1.5
HY_TARGET = 1e-2

S5_GROUP = 16
S5_GROUPS = D_MODEL // S5_GROUP
S5_STATE = 64
SCAN_CHUNK = 128

ML_INNER = 2 * D_MODEL
ML_HEADS = 4
ML_HEAD_DIM = ML_INNER // ML_HEADS
ML_QKV_BLOCK = 4
ML_CONV = 3
ML_CHUNK = 64

N_EXPERTS = 32
TOP_K = 4
D_EXPERT = D_MODEL
SWIGLU_LIMIT = 7.0
SWIGLU_ALPHA = 1.702
MOE_BLOCK = 512

kernel_name = 'hybrid_hyena_s5_mlstm_moe_dit'


def _rmsnorm(x, g):
    xf = x.astype(jnp.float32)
    y = xf * lax.rsqrt(jnp.mean(xf * xf, axis=-1, keepdims=True) + EPS)
    return (y * g.astype(jnp.float32)).astype(x.dtype)


def _flip(t, rev):
    return t[:, ::-1] if rev else t


def _dwconv(u, w, b):
    width = w.shape[0]
    pad = width // 2
    L = u.shape[1]
    up = jnp.pad(u, ((0, 0), (pad, pad), (0, 0)))
    out = up[:, 0:L] * w[0]
    for j in range(1, width):
        out = out + up[:, j:j + L] * w[j]
    return out + b


def _blockdiag(x, w):
    bsz, L, _ = x.shape
    nb, bs, _ = w.shape
    return jnp.einsum('blni,nio->blno', x.reshape(bsz, L, nb, bs), w).reshape(bsz, L, nb * bs)


def _hyena_filter(L, f_w1, f_b1, f_w2, f_b2, f_w3, f_freq):
    f32 = jnp.float32
    d = f_w3.shape[1] // 2
    t = jnp.linspace(0.0, 1.0, L, dtype=f32)[:, None]
    w = (2.0 * math.pi / L) * jnp.arange(L, dtype=f32)[:, None]
    f = jnp.linspace(1e-4, HY_BANDS - 1, HY_BANDS, dtype=f32)[None, :]
    z = jnp.concatenate([t, jnp.cos(f * w), -jnp.sin(f * w)], axis=-1)
    freq = f_freq.astype(f32)
    hid = jnp.sin(freq * (z @ f_w1.astype(f32) + f_b1.astype(f32)))
    hid = jnp.sin(freq * (hid @ f_w2.astype(f32) + f_b2.astype(f32)))
    filt = hid @ f_w3.astype(f32)
    deltas = jnp.abs(jnp.linspace(math.log(HY_FAST_DECAY) / HY_TARGET,
                                  math.log(HY_SLOW_DECAY) / HY_TARGET, d, dtype=f32))
    decay = jnp.exp(-t * deltas)
    fwd = filt[:, :d] * decay
    bwd = filt[:, d:] * decay
    full = jnp.concatenate([fwd, jnp.zeros((1, d), f32), bwd[:0:-1]], axis=0)
    return full / jnp.sum(jnp.abs(full), axis=0, keepdims=True)


def _hyena(h, w_in, b_in, conv_w, conv_b, f_w1, f_b1, f_w2, f_b2, f_w3, f_freq, f_bias, w_out, b_out):
    L = h.shape[1]
    u = _dwconv(h @ w_in + b_in, conv_w, conv_b)
    x0, x1, v = jnp.split(u, 3, axis=-1)
    z = (v * x1).astype(jnp.float32)
    filt = _hyena_filter(L, f_w1, f_b1, f_w2, f_b2, f_w3, f_freq)
    zf = jnp.fft.rfft(z, n=2 * L, axis=1)
    ff = jnp.fft.rfft(filt, n=2 * L, axis=0)
    y = jnp.fft.irfft(zf * ff[None], n=2 * L, axis=1)[:, :L] + z * f_bias.astype(jnp.float32)
    y = y.astype(h.dtype) * x0
    return y @ w_out + b_out


def _lin_combine(e1, e2):
    a1, b1 = e1
    a2, b2 = e2
    return a1 * a2, a2 * b1 + b2


def _s5_scan(u, lam_bar, b_bar, c_mat, st0, with_out):
    bsz, L, _ = u.shape
    n = L // SCAN_CHUNK
    ub = jnp.moveaxis(u.astype(jnp.float32).reshape(bsz, n, SCAN_CHUNK, S5_GROUPS, S5_GROUP), 1, 0)

    def step(st, u_blk):
        bu = jnp.einsum('btgh,gph->btgp', u_blk.astype(jnp.complex64), b_bar)
        bu = bu.at[:, 0].add(lam_bar * st)
        _, states = lax.associative_scan(_lin_combine, (jnp.broadcast_to(lam_bar, bu.shape), bu), axis=1)
        y = jnp.einsum('btgp,ghp->btgh', states, c_mat).real if with_out else None
        return states[:, -1], y

    st, y = lax.scan(step, st0, ub)
    if with_out:
        y = jnp.moveaxis(y, 0, 1).reshape(bsz, L, S5_GROUPS * S5_GROUP)
    return y, st


def _s5(h_ctx, h_lat, a_re, a_im, log_step, b_re, b_im, c_re, c_im, d_skip, w1, b1, w2, b2, ctx_out):
    f32 = jnp.float32
    bsz = h_lat.shape[0]
    outs_c, outs_l = [], []
    for dr in range(2):
        rev = dr == 1
        lam = lax.complex(a_re[dr].astype(f32), a_im[dr].astype(f32))
        lam_bar = jnp.exp(lam * jnp.exp(log_step[dr].astype(f32))[:, None])
        b_bar = ((lam_bar - 1.0) / lam)[..., None] * lax.complex(b_re[dr].astype(f32), b_im[dr].astype(f32))
        c_mat = lax.complex(c_re[dr].astype(f32), c_im[dr].astype(f32))
        st0 = jnp.zeros((bsz, S5_GROUPS, S5_STATE), jnp.complex64)
        yc, st = _s5_scan(_flip(h_ctx, rev), lam_bar, b_bar, c_mat, st0, ctx_out)
        yl, _ = _s5_scan(_flip(h_lat, rev), lam_bar, b_bar, c_mat, st, True)
        outs_l.append(_flip(yl, rev))
        if ctx_out:
            outs_c.append(_flip(yc, rev))

    def finish(y, h):
        y = jax.nn.gelu(y + d_skip.astype(f32) * h.astype(f32)).astype(h.dtype)
        return (y @ w1 + b1) * jax.nn.sigmoid(y @ w2 + b2)

    y_lat = finish(outs_l[0] + outs_l[1], h_lat)
    y_ctx = finish(outs_c[0] + outs_c[1], h_ctx) if ctx_out else None
    return y_ctx, y_lat


def _mlstm_scan(q, k, v, ig, fg, carry, with_out):
    bsz, L, nh, dh = q.shape
    n = L // ML_CHUNK

    def blocks(t):
        return jnp.moveaxis(t.reshape((bsz, n, ML_CHUNK) + t.shape[2:]), (1, 2), (0, 3))

    tril = jnp.tril(jnp.ones((ML_CHUNK, ML_CHUNK), dtype=bool))

    def step(carry, blk):
        c_mat, n_vec, m = carry
        qb, kb, vb, ib, fb = blk
        b = jnp.cumsum(jax.nn.log_sigmoid(fb), axis=-1)
        b_tot = b[..., -1]
        w_log = b_tot[..., None] - b + ib
        m_new = jnp.maximum(b_tot + m, jnp.max(w_log, axis=-1))
        w_in = jnp.exp(w_log - m_new[..., None])
        dec = jnp.exp(b_tot + m - m_new)
        c_new = dec[..., None, None] * c_mat + jnp.einsum('bhsd,bhse->bhde', w_in[..., None] * kb, vb)
        n_new = dec[..., None] * n_vec + jnp.einsum('bhs,bhsd->bhd', w_in, kb)
        if not with_out:
            return (c_new, n_new, m_new), None
        log_d = jnp.where(tril, b[..., :, None] - b[..., None, :] + ib[..., None, :], -jnp.inf)
        inter = b + m[..., None]
        m_row = jnp.maximum(inter, jnp.max(log_d, axis=-1))
        s = jnp.einsum('bhtd,bhsd->bhts', qb, kb) * jnp.exp(log_d - m_row[..., None])
        w_prev = jnp.exp(inter - m_row)
        num = w_prev[..., None] * jnp.einsum('bhtd,bhde->bhte', qb, c_mat) + jnp.einsum('bhts,bhse->bhte', s, vb)
        den = w_prev * jnp.einsum('bhtd,bhd->bht', qb, n_vec) + jnp.sum(s, axis=-1)
        hb = num / jnp.maximum(jnp.abs(den), jnp.exp(-m_row))[..., None]
        return (c_new, n_new, m_new), hb

    carry, hs = lax.scan(step, carry, tuple(blocks(t) for t in (q, k, v, ig, fg)))
    if with_out:
        hs = jnp.moveaxis(hs, (0, 3), (1, 2)).reshape(bsz, L, nh * dh)
    return hs, carry


def _head_layernorm(h, g):
    bsz, L, _ = h.shape
    hh = h.reshape(bsz, L, ML_HEADS, ML_HEAD_DIM)
    mu = jnp.mean(hh, axis=-1, keepdims=True)
    var = jnp.mean(jnp.square(hh - mu), axis=-1, keepdims=True)
    return ((hh - mu) * lax.rsqrt(var + EPS)).reshape(bsz, L, -1) * g.astype(jnp.float32)


def _mlstm(h_ctx, h_lat, w_up, conv_w, conv_b, w_q, w_k, w_v, ig_w, ig_b, fg_w, fg_b, onorm_g, skip, w_down, ctx_out):
    f32 = jnp.float32

    def project(h):
        bsz, L, _ = h.shape
        xm = h @ w_up[:, :ML_INNER]
        xc = jax.nn.silu(_dwconv(xm, conv_w, conv_b))
        q, k, v = _blockdiag(xc, w_q), _blockdiag(xc, w_k), _blockdiag(xm, w_v)
        qkv = jnp.concatenate([q, k, v], axis=-1)
        heads = lambda t: t.astype(f32).reshape(bsz, L, ML_HEADS, ML_HEAD_DIM)
        igs = [(qkv @ ig_w[dr] + ig_b[dr]).astype(f32) for dr in range(2)]
        fgs = [(qkv @ fg_w[dr] + fg_b[dr]).astype(f32) for dr in range(2)]
        return xc, (heads(q), heads(k) * ML_HEAD_DIM ** -0.5, heads(v)), igs, fgs

    def finish(hsum, h, xc):
        hn = _head_layernorm(hsum, onorm_g).astype(h.dtype)
        z = h @ w_up[:, ML_INNER:]
        return ((hn + skip * xc) * jax.nn.silu(z)) @ w_down

    xc_c, qkv_c, ig_c, fg_c = project(h_ctx)
    xc_l, qkv_l, ig_l, fg_l = project(h_lat)
    bsz = h_lat.shape[0]
    outs_c, outs_l = [], []
    for dr in range(2):
        rev = dr == 1
        carry = (jnp.zeros((bsz, ML_HEADS, ML_HEAD_DIM, ML_HEAD_DIM), f32),
                 jnp.zeros((bsz, ML_HEADS, ML_HEAD_DIM), f32),
                 jnp.zeros((bsz, ML_HEADS), f32))
        hc, carry = _mlstm_scan(*[_flip(t, rev) for t in (*qkv_c, ig_c[dr], fg_c[dr])], carry, ctx_out)
        hl, _ = _mlstm_scan(*[_flip(t, rev) for t in (*qkv_l, ig_l[dr], fg_l[dr])], carry, True)
        outs_l.append(_flip(hl, rev))
        if ctx_out:
            outs_c.append(_flip(hc, rev))
    y_lat = finish(outs_l[0] + outs_l[1], h_lat, xc_l)
    y_ctx = finish(outs_c[0] + outs_c[1], h_ctx, xc_c) if ctx_out else None
    return y_ctx, y_lat


def _moe(h, router_w, router_b, w_gu, b_gu, w_down, b_down):
    n_tok = h.shape[0]
    logits = (h @ router_w + router_b).astype(jnp.float32)
    top_val, top_idx = lax.top_k(logits, TOP_K)
    gate = jax.nn.softmax(top_val, axis=-1)
    n_rows = n_tok * TOP_K
    n_blocks = -(-n_rows // MOE_BLOCK) + N_EXPERTS
    flat_e = top_idx.reshape(-1)
    order = jnp.argsort(flat_e)
    sorted_e = flat_e[order]
    counts = jnp.bincount(flat_e, length=N_EXPERTS)
    padded = (counts + MOE_BLOCK - 1) // MOE_BLOCK * MOE_BLOCK
    pad_end = jnp.cumsum(padded)
    rank = jnp.arange(n_rows) - (jnp.cumsum(counts) - counts)[sorted_e]
    dest = (pad_end - padded)[sorted_e] + rank
    slot_tok = jnp.full((n_blocks * MOE_BLOCK,), n_tok, jnp.int32).at[dest].set((order // TOP_K).astype(jnp.int32))
    slot_gate = jnp.zeros((n_blocks * MOE_BLOCK,), jnp.float32).at[dest].set(gate.reshape(-1)[order])
    block_e = jnp.minimum(jnp.searchsorted(pad_end, jnp.arange(n_blocks) * MOE_BLOCK, side='right'), N_EXPERTS - 1)

    def step(out, blk):
        tok, g, e = blk
        xb = h[jnp.minimum(tok, n_tok - 1)]
        gu = xb @ w_gu[e] + b_gu[e]
        gt = jnp.minimum(gu[:, 0::2], SWIGLU_LIMIT)
        up = jnp.clip(gu[:, 1::2], -SWIGLU_LIMIT, SWIGLU_LIMIT)
        yb = ((up + 1.0) * gt * jax.nn.sigmoid(SWIGLU_ALPHA * gt)) @ w_down[e] + b_down[e]
        return out.at[tok].add((yb * g[:, None]).astype(out.dtype), mode='drop'), None

    out, _ = lax.scan(step, jnp.zeros_like(h),
                      (slot_tok.reshape(n_blocks, MOE_BLOCK), slot_gate.reshape(n_blocks, MOE_BLOCK), block_e))
    return out


def setup_inputs(seed: int = 0) -> dict:
    key = jax.random.key(seed)
    keys = iter(jax.random.split(key, 96))

    def nrm(shape, scale):
        return jax.random.normal(next(keys), shape, jnp.float32) * scale

    def gain(shape):
        return 1.0 + nrm(shape, 0.02)

    D, E, F = D_MODEL, N_EXPERTS, D_EXPERT
    G, P, H = S5_GROUPS, S5_STATE, S5_GROUP
    I = ML_INNER
    return {
        'x': nrm((BATCH, SEQ, D), 1.0),
        'c': nrm((BATCH, D), 1.0),
        'ctx': nrm((BATCH, CTX_LEN, D), 1.0),
        'c_ctx': nrm((D,), 1.0),
        'ada_w': nrm((DEPTH, D, 6 * D), 0.5 * D ** -0.5),
        'ada_b': nrm((DEPTH, 6 * D), 0.02),
        'norm1_g': gain((DEPTH, D)),
        'norm2_g': gain((DEPTH, D)),
        'router_w': nrm((DEPTH, D, E), D ** -0.5),
        'router_b': nrm((DEPTH, E), 0.01),
        'moe_w_gu': nrm((DEPTH, E, D, 2 * F), D ** -0.5),
        'moe_b_gu': nrm((DEPTH, E, 2 * F), 0.01),
        'moe_w_down': nrm((DEPTH, E, F, D), F ** -0.5),
        'moe_b_down': nrm((DEPTH, E, D), 0.01),
        'hy_w_in': nrm((N_HYENA, D, 3 * D), D ** -0.5),
        'hy_b_in': nrm((N_HYENA, 3 * D), 0.01),
        'hy_conv_w': nrm((N_HYENA, HY_CONV, 3 * D), HY_CONV ** -0.5),
        'hy_conv_b': nrm((N_HYENA, 3 * D), 0.01),
        'hy_f_w1': nrm((N_HYENA, HY_EMB, HY_ORDER), HY_EMB ** -0.5),
        'hy_f_b1': nrm((N_HYENA, HY_ORDER), 0.1),
        'hy_f_w2': nrm((N_HYENA, HY_ORDER, HY_ORDER), HY_ORDER ** -0.5),
        'hy_f_b2': nrm((N_HYENA, HY_ORDER), 0.1),
        'hy_f_w3': nrm((N_HYENA, HY_ORDER, 2 * D), HY_ORDER ** -0.5),
        'hy_f_freq': gain((N_HYENA, HY_ORDER)),
        'hy_f_bias': nrm((N_HYENA, D), 1.0),
        'hy_w_out': nrm((N_HYENA, D, D), D ** -0.5),
        'hy_b_out': nrm((N_HYENA, D), 0.01),
        's5_a_re': -0.5 + nrm((N_S5, 2, G, P), 0.01),
        's5_a_im': jnp.broadcast_to(math.pi * jnp.arange(P, dtype=jnp.float32), (N_S5, 2, G, P)),
        's5_log_step': jax.random.uniform(next(keys), (N_S5, 2, G), jnp.float32,
                                          minval=math.log(1e-3), maxval=math.log(1e-1)),
        's5_b_re': nrm((N_S5, 2, G, P, H), (2 * H) ** -0.5),
        's5_b_im': nrm((N_S5, 2, G, P, H), (2 * H) ** -0.5),
        's5_c_re': nrm((N_S5, 2, G, H, P), P ** -0.5),
        's5_c_im': nrm((N_S5, 2, G, H, P), P ** -0.5),
        's5_d': nrm((N_S5, D), 1.0),
        's5_glu_w1': nrm((N_S5, D, D), D ** -0.5),
        's5_glu_b1': nrm((N_S5, D), 0.01),
        's5_glu_w2': nrm((N_S5, D, D), D ** -0.5),
        's5_glu_b2': nrm((N_S5, D), 0.01),
        'ml_w_up': nrm((N_MLSTM, D, 2 * I), D ** -0.5),
        'ml_conv_w': nrm((N_MLSTM, ML_CONV, I), ML_CONV ** -0.5),
        'ml_conv_b': nrm((N_MLSTM, I), 0.01),
        'ml_w_q': nrm((N_MLSTM, I // ML_QKV_BLOCK, ML_QKV_BLOCK, ML_QKV_BLOCK), ML_QKV_BLOCK ** -0.5),
        'ml_w_k': nrm((N_MLSTM, I // ML_QKV_BLOCK, ML_QKV_BLOCK, ML_QKV_BLOCK), ML_QKV_BLOCK ** -0.5),
        'ml_w_v': nrm((N_MLSTM, I // ML_QKV_BLOCK, ML_QKV_BLOCK, ML_QKV_BLOCK), ML_QKV_BLOCK ** -0.5),
        'ml_ig_w': nrm((N_MLSTM, 2, 3 * I, ML_HEADS), 0.1 * (3 * I) ** -0.5),
        'ml_ig_b': nrm((N_MLSTM, 2, ML_HEADS), 0.1),
        'ml_fg_w': nrm((N_MLSTM, 2, 3 * I, ML_HEADS), 0.1 * (3 * I) ** -0.5),
        'ml_fg_b': jnp.linspace(3.0, 6.0, ML_HEADS, dtype=jnp.float32) + nrm((N_MLSTM, 2, ML_HEADS), 0.1),
        'ml_onorm_g': gain((N_MLSTM, I)),
        'ml_skip': gain((N_MLSTM, I)),
        'ml_w_down': nrm((N_MLSTM, I, D), I ** -0.5),
        'final_norm_g': gain((D,)),
    }


def reference(x, c, ctx, c_ctx, ada_w, ada_b, norm1_g, norm2_g, router_w, router_b,
              moe_w_gu, moe_b_gu, moe_w_down, moe_b_down,
              hy_w_in, hy_b_in, hy_conv_w, hy_conv_b, hy_f_w1, hy_f_b1, hy_f_w2, hy_f_b2,
              hy_f_w3, hy_f_freq, hy_f_bias, hy_w_out, hy_b_out,
              s5_a_re, s5_a_im, s5_log_step, s5_b_re, s5_b_im, s5_c_re, s5_c_im, s5_d,
              s5_glu_w1, s5_glu_b1, s5_glu_w2, s5_glu_b2,
              ml_w_up, ml_conv_w, ml_conv_b, ml_w_q, ml_w_k, ml_w_v, ml_ig_w, ml_ig_b,
              ml_fg_w, ml_fg_b, ml_onorm_g, ml_skip, ml_w_down,
              final_norm_g):
    bsz, seq, d = x.shape
    n_ctx = ctx.shape[1]
    rows = seq // GRID_W
    sc_lat = jax.nn.silu(c)
    sc_ctx = jax.nn.silu(c_ctx)[None]
    for i in range(DEPTH):
        kind, j = i % N_MIXERS, i // N_MIXERS
        ctx_live = any(l % N_MIXERS != 0 for l in range(i + 1, DEPTH))
        need_hc = ctx_live or kind != 0
        mod = jnp.split((sc_lat @ ada_w[i] + ada_b[i])[:, None, :], 6, axis=-1)
        h = _rmsnorm(x, norm1_g[i]) * (1 + mod[1]) + mod[0]
        if need_hc:
            mod_c = jnp.split((sc_ctx @ ada_w[i] + ada_b[i])[:, None, :], 6, axis=-1)
            hc = _rmsnorm(ctx, norm1_g[i]) * (1 + mod_c[1]) + mod_c[0]
        if kind == 0:
            hp = (hy_w_in[j], hy_b_in[j], hy_conv_w[j], hy_conv_b[j], hy_f_w1[j], hy_f_b1[j],
                  hy_f_w2[j], hy_f_b2[j], hy_f_w3[j], hy_f_freq[j], hy_f_bias[j], hy_w_out[j], hy_b_out[j])
            y = _hyena(h, *hp)
            yc = _hyena(hc, *hp) if ctx_live else None
        elif kind == 1:
            yc, y = _s5(hc, h, s5_a_re[j], s5_a_im[j], s5_log_step[j], s5_b_re[j], s5_b_im[j],
                        s5_c_re[j], s5_c_im[j], s5_d[j], s5_glu_w1[j], s5_glu_b1[j],
                        s5_glu_w2[j], s5_glu_b2[j], ctx_live)
        else:
            h_cm = h.reshape(bsz, rows, GRID_W, d).transpose(0, 2, 1, 3).reshape(bsz, seq, d)
            yc, y_cm = _mlstm(hc, h_cm, ml_w_up[j], ml_conv_w[j], ml_conv_b[j], ml_w_q[j], ml_w_k[j],
                              ml_w_v[j], ml_ig_w[j], ml_ig_b[j], ml_fg_w[j], ml_fg_b[j],
                              ml_onorm_g[j], ml_skip[j], ml_w_down[j], ctx_live)
            y = y_cm.reshape(bsz, GRID_W, rows, d).transpose(0, 2, 1, 3).reshape(bsz, seq, d)
        x = x + mod[2] * y
        h2 = _rmsnorm(x, norm2_g[i]) * (1 + mod[4]) + mod[3]
        mp = (router_w[i], router_b[i], moe_w_gu[i], moe_b_gu[i], moe_w_down[i], moe_b_down[i])
        if ctx_live:
            ctx = ctx + mod_c[2] * yc
            hc2 = _rmsnorm(ctx, norm2_g[i]) * (1 + mod_c[4]) + mod_c[3]
            out = _moe(jnp.concatenate([hc2.reshape(-1, d), h2.reshape(-1, d)], axis=0), *mp)
            ctx = ctx + mod_c[5] * out[:bsz * n_ctx].reshape(ctx.shape)
            x = x + mod[5] * out[bsz * n_ctx:].reshape(x.shape)
        else:
            x = x + mod[5] * _moe(h2.reshape(-1, d), *mp).reshape(x.shape)
    return _rmsnorm(x, final_norm_g)
```

```python
import functools
import math

import numpy as np
import jax
import jax.numpy as jnp
from jax import lax
from jax.experimental import pallas as pl
from jax.experimental.pallas import tpu as pltpu

F32 = jnp.float32
BF16 = jnp.bfloat16
I32 = jnp.int32

EPS = 1e-6
GRID_W = 64
TOP_K = 4
SWIGLU_LIMIT = 7.0
SWIGLU_ALPHA = 1.702
HY_BANDS = 16
HY_FAST_DECAY = 0.3
HY_SLOW_DECAY = 1.5
HY_TARGET = 1e-2
S5_SUB = 16
ML_CHUNK = 256
MOE_BM = 512
V7X_VMEM_LIMIT_BYTES = 56 * 1024 * 1024


def _cparams(n_axes):
    return pltpu.CompilerParams(dimension_semantics=("arbitrary",) * n_axes,
                                vmem_limit_bytes=V7X_VMEM_LIMIT_BYTES)


def _dot(a, b):
    return jnp.dot(a, b, preferred_element_type=F32)


def _split(a):
    hi = a.astype(BF16)
    lo = (a - hi.astype(F32)).astype(BF16)
    return hi, lo


def _dot3(a, b):
    ah, al = _split(a)
    bh, bl = _split(b)
    return _dot(ah, bh) + (_dot(ah, bl) + _dot(al, bh))


def _norm_mod(x, g, shift, scale):
    ms = jnp.mean(x * x, axis=-1, keepdims=True)
    return (x * lax.rsqrt(ms + EPS)) * g * (1.0 + scale) + shift


def _silu(x):
    return x * jax.nn.sigmoid(x)


def _resident(shape):
    return pl.BlockSpec(shape, lambda *_: (0,) * len(shape), pipeline_mode=pl.Buffered(1))


def _rows(tm, n, off=0):
    return pl.BlockSpec((tm, n), lambda i: (i + off, 0))


def _mod_spec(layer, six_d, tiles_per_batch, ctx_row):
    if ctx_row is not None:
        return pl.BlockSpec((1, 1, 1, six_d), lambda i: (layer, ctx_row, 0, 0))
    return pl.BlockSpec((1, 1, 1, six_d), lambda i: (layer, i // tiles_per_batch, 0, 0))


def _adaln_kernel(c_ref, w_ref, b_ref, o_ref):
    o_ref[0] = _dot3(_silu(c_ref[...]), w_ref[0]) + b_ref[0]


def _adaln(c_all, ada_w, ada_b):
    depth, d, n6 = ada_w.shape
    rows = c_all.shape[0]
    tn = n6 // 4
    out = pl.pallas_call(
        _adaln_kernel,
        out_shape=jax.ShapeDtypeStruct((depth, rows, n6), F32),
        grid=(depth, n6 // tn),
        in_specs=[pl.BlockSpec((rows, d), lambda l, j: (0, 0)),
                  pl.BlockSpec((1, d, tn), lambda l, j: (l, 0, j)),
                  pl.BlockSpec((1, 1, tn), lambda l, j: (l, 0, j))],
        out_specs=pl.BlockSpec((1, rows, tn), lambda l, j: (l, 0, j)),
        compiler_params=_cparams(2),
        name="adaln",
    )(c_all, ada_w, ada_b.reshape(depth, 1, n6))
    return out.reshape(depth, rows, 1, n6)


def _hy_in_kernel(*refs, halo, tiles_per_seq, d):
    if halo:
        x_ref, xp_ref, xn_ref, mod_ref, g_ref, w_ref, b_ref, cw_ref, cb_ref, x0_ref, z_ref, p_ref = refs
    else:
        x_ref, mod_ref, g_ref, w_ref, b_ref, cw_ref, cb_ref, x0_ref, z_ref, p_ref = refs
    tm = x_ref.shape[0]
    m = mod_ref[0, 0]
    shift, scale = m[:, :d], m[:, d:2 * d]
    if halo:
        xv = jnp.concatenate([xp_ref[...], x_ref[...], xn_ref[...]], axis=0)
    else:
        xv = x_ref[...]
    h = _norm_mod(xv, g_ref[...], shift, scale)
    p = _dot(h.astype(BF16), w_ref[...]) + b_ref[...]
    if halo:
        p_ref[...] = p
        j = pl.program_id(0) % tiles_per_seq
        keep_prev = (j != 0).astype(F32)
        keep_next = (j != tiles_per_seq - 1).astype(F32)
        p_ref[0:8, :] = p_ref[0:8, :] * keep_prev
        p_ref[8 + tm:16 + tm, :] = p_ref[8 + tm:16 + tm, :] * keep_next
    else:
        p_ref[8:8 + tm, :] = p
        p_ref[0:8, :] = jnp.zeros((8, 3 * d), F32)
        p_ref[8 + tm:16 + tm, :] = jnp.zeros((8, 3 * d), F32)

    def conv(c0):
        cs = slice(c0, c0 + d)
        return (p_ref[7:7 + tm, cs] * cw_ref[0:1, cs] + p_ref[8:8 + tm, cs] * cw_ref[1:2, cs]
                + p_ref[9:9 + tm, cs] * cw_ref[2:3, cs] + cb_ref[:, cs])

    x0_ref[...] = conv(0)
    z_ref[...] = conv(2 * d) * conv(d)


def _hy_in(x, mods, layer, norm_g, w_in, b_in, conv_w, conv_b, seq, ctx):
    n, d = x.shape
    tm = min(seq, 512)
    tps = seq // tm
    halo = tps > 1
    n8 = n // 8
    in_specs = [_rows(tm, d)]
    args = [x]
    if halo:
        in_specs += [pl.BlockSpec((8, d), lambda i: (jnp.maximum(i * (tm // 8) - 1, 0), 0)),
                     pl.BlockSpec((8, d), lambda i: (jnp.minimum((i + 1) * (tm // 8), n8 - 1), 0))]
        args += [x, x]
    in_specs += [_mod_spec(layer, 6 * d, tps, ctx), _resident((1, d)), _resident((d, 3 * d)),
                 _resident((1, 3 * d)), _resident((3, 3 * d)), _resident((1, 3 * d))]
    args += [mods, norm_g.reshape(1, d), w_in.astype(BF16), b_in.reshape(1, 3 * d), conv_w,
             conv_b.reshape(1, 3 * d)]
    return pl.pallas_call(
        functools.partial(_hy_in_kernel, halo=halo, tiles_per_seq=tps, d=d),
        out_shape=(jax.ShapeDtypeStruct((n, d), F32), jax.ShapeDtypeStruct((n, d), F32)),
        grid=(n // tm,),
        in_specs=in_specs,
        out_specs=(_rows(tm, d), _rows(tm, d)),
        scratch_shapes=[pltpu.VMEM((tm + 16, 3 * d), F32)],
        compiler_params=_cparams(1),
        name="hyena_in",
    )(*args)


def _dft_tables(length):
    n2 = 2 * length
    kn = (np.arange(length, dtype=np.int64)[:, None] * np.arange(length, dtype=np.int64)[None, :]) % n2
    ang = kn.astype(np.float64) * (2.0 * np.pi / n2)
    return jnp.asarray(np.cos(ang), dtype=BF16), jnp.asarray(np.sin(ang), dtype=BF16)


def _hy_feats(length):
    t = np.linspace(0.0, 1.0, length, dtype=np.float32)[:, None]
    w = (np.float32(2.0 * math.pi / length) * np.arange(length, dtype=np.float32))[:, None]
    f = np.linspace(1e-4, HY_BANDS - 1, HY_BANDS, dtype=np.float32)[None, :]
    z = np.concatenate([t, np.cos(f * w), -np.sin(f * w)], axis=-1).astype(np.float32)
    zp = np.zeros((length, 128), np.float32)
    zp[:, :z.shape[1]] = z
    return jnp.asarray(zp)


def _hy_deltas(d):
    return jnp.asarray(np.abs(np.linspace(math.log(HY_FAST_DECAY) / HY_TARGET,
                                          math.log(HY_SLOW_DECAY) / HY_TARGET, d,
                                          dtype=np.float32))[None, :])


def _hy_filter_kernel(zf_ref, w1_ref, b1_ref, w2_ref, b2_ref, fr_ref, w3f_ref, w3b_ref, dl_ref,
                      c_ref, s_ref, hr_ref, hi_ref, hn_ref):
    zf = zf_ref[...]
    freq = fr_ref[...]
    hid = jnp.sin(freq * (_dot3(zf, w1_ref[...]) + b1_ref[...]))
    hid = jnp.sin(freq * (_dot3(hid, w2_ref[...]) + b2_ref[...]))
    decay = jnp.exp(-zf[:, 0:1] * dl_ref[...])
    fwd = _dot3(hid, w3f_ref[...]) * decay
    bwd = _dot3(hid, w3b_ref[...]) * decay
    row = lax.broadcasted_iota(I32, fwd.shape, 0)
    bwd = jnp.where(row == 0, 0.0, bwd)
    inv = 1.0 / (jnp.sum(jnp.abs(fwd), axis=0, keepdims=True) + jnp.sum(jnp.abs(bwd), axis=0, keepdims=True))
    length = fwd.shape[0]
    a = (fwd + bwd) * inv
    bm = (bwd - fwd) * inv
    ah, al = _split(a)
    bh, bl = _split(bm)
    c = c_ref[...]
    s = s_ref[...]
    wk = jnp.where(row == 0, 0.5 / length, 1.0 / length)
    hr_ref[...] = (_dot(c, ah) + _dot(c, al)) * wk
    hi_ref[...] = (_dot(s, bh) + _dot(s, bl)) * wk
    sgn = (1 - 2 * (row & 1)).astype(F32)
    hn_ref[...] = jnp.sum(a * sgn, axis=0, keepdims=True) * (0.5 / length)


def _hy_filter(length, f_w1, f_b1, f_w2, f_b2, f_w3, f_freq, cmat, smat):
    order = f_w1.shape[1]
    d = f_w3.shape[1] // 2
    dt = min(d, 256)
    nd = d // dt
    w1p = jnp.zeros((128, order), F32).at[:f_w1.shape[0]].set(f_w1)
    full = lambda shape: pl.BlockSpec(shape, lambda j: (0,) * len(shape))
    return pl.pallas_call(
        _hy_filter_kernel,
        out_shape=(jax.ShapeDtypeStruct((length, d), F32), jax.ShapeDtypeStruct((length, d), F32),
                   jax.ShapeDtypeStruct((1, d), F32)),
        grid=(nd,),
        in_specs=[full((length, 128)), full((128, order)), full((1, order)), full((order, order)),
                  full((1, order)), full((1, order)),
                  pl.BlockSpec((order, dt), lambda j: (0, j)),
                  pl.BlockSpec((order, dt), lambda j: (0, j + nd)),
                  pl.BlockSpec((1, dt), lambda j: (0, j)),
                  _resident((length, length)), _resident((length, length))],
        out_specs=(pl.BlockSpec((length, dt), lambda j: (0, j)), pl.BlockSpec((length, dt), lambda j: (0, j)),
                   pl.BlockSpec((1, dt), lambda j: (0, j))),
        compiler_params=_cparams(1),
        name="hyena_filter",
    )(_hy_feats(length), w1p, f_b1.reshape(1, order), f_w2, f_b2.reshape(1, order),
      f_freq.reshape(1, order), f_w3, f_w3, _hy_deltas(d), cmat, smat)


def _hy_conv_kernel(z_ref, x0_ref, hr_ref, hi_ref, hn_ref, fb_ref, c_ref, s_ref, o_ref):
    z = z_ref[...]
    zb = z.astype(BF16)
    c = c_ref[...]
    s = s_ref[...]
    zr = _dot(c, zb)
    zs = _dot(s, zb)
    hr = hr_ref[...]
    hi = hi_ref[...]
    yr = zr * hr + zs * hi
    yi = zr * hi - zs * hr
    row = lax.broadcasted_iota(I32, z.shape, 0)
    sgn = (1 - 2 * (row & 1)).astype(F32)
    zn = jnp.sum(z * sgn, axis=0, keepdims=True)
    y = _dot(c, yr.astype(BF16)) - _dot(s, yi.astype(BF16)) + sgn * (zn * hn_ref[...])
    y = y + z * fb_ref[...]
    o_ref[...] = (y * x0_ref[...]).astype(o_ref.dtype)


def _hy_conv(z, x0, hr, hi, hn, f_bias, cmat, smat, seq):
    n, d = z.shape
    bsz = n // seq
    dt = min(d, 256)
    tile = pl.BlockSpec((seq, dt), lambda j, b: (b, j))
    filt = pl.BlockSpec((seq, dt), lambda j, b: (0, j))
    vec = pl.BlockSpec((1, dt), lambda j, b: (0, j))
    return pl.pallas_call(
        _hy_conv_kernel,
        out_shape=jax.ShapeDtypeStruct((n, d), BF16),
        grid=(d // dt, bsz),
        in_specs=[tile, tile, filt, filt, vec, vec, _resident((seq, seq)), _resident((seq, seq))],
        out_specs=tile,
        compiler_params=_cparams(2),
        name="hyena_conv",
    )(z, x0, hr, hi, hn, f_bias.reshape(1, d), cmat, smat)


def _proj_res_kernel(y_ref, x_ref, mod_ref, w_ref, b_ref, o_ref, *, d):
    gate = mod_ref[0, 0][:, 2 * d:3 * d]
    o_ref[...] = x_ref[...] + gate * (_dot(y_ref[...], w_ref[...]) + b_ref[...])


def _proj_res(y, x, mods, layer, w, b, seq, ctx):
    n, d = x.shape
    k = y.shape[1]
    tm = min(seq, 512)
    return pl.pallas_call(
        functools.partial(_proj_res_kernel, d=d),
        out_shape=jax.ShapeDtypeStruct((n, d), F32),
        grid=(n // tm,),
        in_specs=[_rows(tm, k), _rows(tm, d), _mod_spec(layer, 6 * d, seq // tm, ctx),
                  _resident((k, d)), _resident((1, d))],
        out_specs=_rows(tm, d),
        compiler_params=_cparams(1),
        name="proj_residual",
    )(y, x, mods, w.astype(BF16), b.reshape(1, d))


def _hyena_layer(x, mods, layer, norm_g, hp, seq, ctx):
    (w_in, b_in, conv_w, conv_b, f_w1, f_b1, f_w2, f_b2, f_w3, f_freq, f_bias, w_out, b_out) = hp
    cmat, smat = _dft_tables(seq)
    x0, z = _hy_in(x, mods, layer, norm_g, w_in, b_in, conv_w, conv_b, seq, ctx)
    hr, hi, hn = _hy_filter(seq, f_w1, f_b1, f_w2, f_b2, f_w3, f_freq, cmat, smat)
    yg = _hy_conv(z, x0, hr, hi, hn, f_bias, cmat, smat, seq)
    return _proj_res(yg, x, mods, layer, w_out, b_out, seq, ctx)


def _norm_kernel(x_ref, mod_ref, g_ref, o_ref, *, d):
    m = mod_ref[0, 0]
    o_ref[...] = _norm_mod(x_ref[...], g_ref[...], m[:, :d], m[:, d:2 * d]).astype(o_ref.dtype)


def _norm_rows(x, mods, layer, norm_g, seq, ctx, dtype):
    n, d = x.shape
    tm = min(seq, 512)
    return pl.pallas_call(
        functools.partial(_norm_kernel, d=d),
        out_shape=jax.ShapeDtypeStruct((n, d), dtype),
        grid=(n // tm,),
        in_specs=[_rows(tm, d), _mod_spec(layer, 6 * d, seq // tm, ctx), _resident((1, d))],
        out_specs=_rows(tm, d),
        compiler_params=_cparams(1),
        name="norm_mod",
    )(x, mods, norm_g.reshape(1, d))


def _s5_operators(a_re, a_im, log_step, b_re, b_im, c_re, c_im):
    hp = lax.Precision.HIGHEST
    t = S5_SUB
    step = jnp.exp(log_step)[..., None, None]
    tau = jnp.arange(t + 1, dtype=F32)[None, None, :, None]
    mag = jnp.exp(a_re[:, :, None, :] * step * tau)
    ang = a_im[:, :, None, :] * step * tau
    pw_r, pw_i = mag * jnp.cos(ang), mag * jnp.sin(ang)
    lb_r, lb_i = pw_r[:, :, 1], pw_i[:, :, 1]
    n2 = a_re * a_re + a_im * a_im
    cf_r = ((lb_r - 1.0) * a_re + lb_i * a_im) / n2
    cf_i = (lb_i * a_re - (lb_r - 1.0) * a_im) / n2
    bb_r = cf_r[..., None] * b_re - cf_i[..., None] * b_im
    bb_i = cf_r[..., None] * b_im + cf_i[..., None] * b_re
    w_r = pw_r[:, :, :t, :, None] * bb_r[:, :, None] - pw_i[:, :, :t, :, None] * bb_i[:, :, None]
    w_i = pw_r[:, :, :t, :, None] * bb_i[:, :, None] + pw_i[:, :, :t, :, None] * bb_r[:, :, None]
    kern = (jnp.einsum('dgop,dgtpi->dgtoi', c_re, w_r, precision=hp)
            - jnp.einsum('dgop,dgtpi->dgtoi', c_im, w_i, precision=hp))
    nd, g, _, h, _ = kern.shape
    p = a_re.shape[-1]
    lag = jnp.arange(t)[None, :] - jnp.arange(t)[:, None]
    blocks = jnp.where((lag >= 0)[None, None, :, :, None, None], kern[:, :, jnp.clip(lag, 0, t - 1)], 0.0)
    m_intra = blocks.transpose(0, 1, 2, 5, 3, 4).reshape(nd, g, t * h, t * h)
    m_in_r = w_r[:, :, ::-1].transpose(0, 1, 2, 4, 3).reshape(nd, g, t * h, p)
    m_in_i = w_i[:, :, ::-1].transpose(0, 1, 2, 4, 3).reshape(nd, g, t * h, p)
    cp_r = c_re[:, :, None] * pw_r[:, :, 1:, None, :] - c_im[:, :, None] * pw_i[:, :, 1:, None, :]
    cp_i = c_re[:, :, None] * pw_i[:, :, 1:, None, :] + c_im[:, :, None] * pw_r[:, :, 1:, None, :]
    m_out_r = cp_r.transpose(0, 1, 4, 2, 3).reshape(nd, g, p, t * h)
    m_out_i = (-cp_i).transpose(0, 1, 4, 2, 3).reshape(nd, g, p, t * h)
    lam_r = pw_r[:, :, t][:, :, None, :]
    lam_i = pw_i[:, :, t][:, :, None, :]
    bf = lambda v: v.astype(BF16)
    return bf(m_intra), bf(m_in_r), bf(m_in_i), bf(m_out_r), bf(m_out_i), lam_r, lam_i


def _s5_kernel(u_ref, m_ref, minr_ref, mini_ref, moutr_ref, mouti_ref, lr_ref, li_ref, y_ref,
               sr_ref, si_ref, *, bsz, nchunk):
    u = u_ref[0, 0]
    sr_ref[...] = _dot(u, minr_ref[0, 0])
    si_ref[...] = _dot(u, mini_ref[0, 0])
    lr = lr_ref[0, 0]
    li = li_ref[0, 0]
    p = lr.shape[-1]

    def body(c, carry):
        xr, xi = carry
        rows = pl.ds(pl.multiple_of(c * bsz, bsz), bsz)
        inc_r = sr_ref[rows, :]
        inc_i = si_ref[rows, :]
        sr_ref[rows, :] = xr
        si_ref[rows, :] = xi
        return lr * xr - li * xi + inc_r, lr * xi + li * xr + inc_i

    zero = jnp.zeros((bsz, p), F32)
    lax.fori_loop(0, nchunk, body, (zero, zero))
    y_ref[0, 0] = (_dot(u, m_ref[0, 0]) + _dot(sr_ref[...].astype(BF16), moutr_ref[0, 0])
                   + _dot(si_ref[...].astype(BF16), mouti_ref[0, 0]))


def _s5_scan(u, ops, bsz):
    m_intra, m_in_r, m_in_i, m_out_r, m_out_i, lam_r, lam_i = ops
    nd, g, nr, th = u.shape
    p = lam_r.shape[-1]
    blk = lambda a, b: pl.BlockSpec((1, 1, a, b), lambda i, j: (i, j, 0, 0))
    return pl.pallas_call(
        functools.partial(_s5_kernel, bsz=bsz, nchunk=nr // bsz),
        out_shape=jax.ShapeDtypeStruct((nd, g, nr, th), F32),
        grid=(nd, g),
        in_specs=[blk(nr, th), blk(th, th), blk(th, p), blk(th, p), blk(p, th), blk(p, th),
                  blk(1, p), blk(1, p)],
        out_specs=blk(nr, th),
        scratch_shapes=[pltpu.VMEM((nr, p), F32), pltpu.VMEM((nr, p), F32)],
        compiler_params=_cparams(2),
        name="s5_scan",
    )(u, m_intra, m_in_r, m_in_i, m_out_r, m_out_i, lam_r, lam_i)


def _s5_finish_kernel(y_ref, x_ref, mod_ref, g_ref, ds_ref, w_ref, b_ref, o_ref, *, d):
    m = mod_ref[0, 0]
    x = x_ref[...]
    h = _norm_mod(x, g_ref[...], m[:, :d], m[:, d:2 * d])
    v = y_ref[...] + ds_ref[...] * h
    v = 0.5 * v * (1.0 + jnp.tanh(math.sqrt(2.0 / math.pi) * (v + 0.044715 * (v * v * v))))
    t = _dot(v.astype(BF16), w_ref[...]) + b_ref[...]
    o_ref[...] = x + m[:, 2 * d:3 * d] * (t[:, :d] * jax.nn.sigmoid(t[:, d:]))


def _s5_finish(y, x, mods, layer, norm_g, d_skip, w1, b1, w2, b2, seq, ctx):
    n, d = x.shape
    tm = min(seq, 512)
    w12 = jnp.concatenate([w1, w2], axis=1).astype(BF16)
    b12 = jnp.concatenate([b1, b2]).reshape(1, 2 * d)
    return pl.pallas_call(
        functools.partial(_s5_finish_kernel, d=d),
        out_shape=jax.ShapeDtypeStruct((n, d), F32),
        grid=(n // tm,),
        in_specs=[_rows(tm, d), _rows(tm, d), _mod_spec(layer, 6 * d, seq // tm, ctx), _resident((1, d)),
                  _resident((1, d)), _resident((d, 2 * d)), _resident((1, 2 * d))],
        out_specs=_rows(tm, d),
        compiler_params=_cparams(1),
        name="s5_finish",
    )(y, x, mods, norm_g.reshape(1, d), d_skip.reshape(1, d), w12, b12)


def _s5_layer(x, ctx, mods, layer, norm_g, sp, bsz, seq, n_ctx):
    (a_re, a_im, log_step, b_re, b_im, c_re, c_im, d_skip, w1, b1, w2, b2) = sp
    d = x.shape[1]
    g, hgrp = a_re.shape[1], b_re.shape[-1]
    t = S5_SUB
    h = _norm_rows(x, mods, layer, norm_g, seq, None, BF16).reshape(bsz, seq, d)
    hc = _norm_rows(ctx, mods, layer, norm_g, n_ctx, bsz, BF16).reshape(bsz, n_ctx, d)
    lt = seq + n_ctx
    nc = lt // t
    seqs = jnp.stack([jnp.concatenate([hc, h], axis=1),
                      jnp.concatenate([hc[:, ::-1], h[:, ::-1]], axis=1)])
    u = seqs.reshape(2, bsz, nc, t, g, hgrp).transpose(0, 4, 2, 1, 3, 5).reshape(2, g, nc * bsz, t * hgrp)
    y = _s5_scan(u, _s5_operators(a_re, a_im, log_step, b_re, b_im, c_re, c_im), bsz)
    y = y.reshape(2, g, nc, bsz, t, hgrp).transpose(0, 3, 2, 4, 1, 5).reshape(2, bsz, lt, d)
    y_ctx = y[0, :, :n_ctx] + y[1, :, :n_ctx][:, ::-1]
    y_lat = y[0, :, n_ctx:] + y[1, :, n_ctx:][:, ::-1]
    x_new = _s5_finish(y_lat.reshape(bsz * seq, d), x, mods, layer, norm_g, d_skip, w1, b1, w2, b2, seq, None)
    ctx_new = _s5_finish(y_ctx.reshape(bsz * n_ctx, d), ctx, mods, layer, norm_g, d_skip, w1, b1, w2, b2,
                         n_ctx, bsz)
    return x_new, ctx_new


def _ml_up_kernel(x_ref, mod_ref, g_ref, w_ref, *out_refs, d, inner):
    m = mod_ref[0, 0]
    h = _norm_mod(x_ref[...], g_ref[...], m[:, :d], m[:, d:2 * d]).astype(BF16)
    out_refs[0][...] = _dot(h, w_ref[:, :inner])
    if len(out_refs) > 1:
        out_refs[1][...] = _silu(_dot(h, w_ref[:, inner:]))


def _ml_up(x, mods, layer, norm_g, w_up, seq, ctx, with_gate):
    n, d = x.shape
    inner = w_up.shape[1] // 2
    tm = min(seq, 512)
    n_out = 2 if with_gate else 1
    return pl.pallas_call(
        functools.partial(_ml_up_kernel, d=d, inner=inner),
        out_shape=tuple(jax.ShapeDtypeStruct((n, inner), F32) for _ in range(n_out)),
        grid=(n // tm,),
        in_specs=[_rows(tm, d), _mod_spec(layer, 6 * d, seq // tm, ctx), _resident((1, d)),
                  _resident((d, 2 * inner))],
        out_specs=tuple(_rows(tm, inner) for _ in range(n_out)),
        compiler_params=_cparams(1),
        name="mlstm_up",
    )(x, mods, norm_g.reshape(1, d), w_up.astype(BF16))


def _ml_qkv_kernel(*refs, halo, tiles_per_seq, inner, k_scale, with_xc):
    if halo:
        xm_ref, xp_ref, xn_ref = refs[:3]
        refs = refs[3:]
    else:
        xm_ref = refs[0]
        refs = refs[1:]
    cw_ref, cb_ref, wq_ref, wk_ref, wv_ref, gw_ref, gb_ref = refs[:7]
    outs = refs[7:]
    if with_xc:
        xc_ref, q_ref, k_ref, v_ref, gate_ref, p_ref = outs
    else:
        q_ref, k_ref, v_ref, gate_ref, p_ref = outs
    tm = xm_ref.shape[0]
    xm = xm_ref[...]
    p_ref[8:8 + tm, :] = xm
    if halo:
        j = pl.program_id(0) % tiles_per_seq
        p_ref[0:8, :] = xp_ref[...] * (j != 0).astype(F32)
        p_ref[8 + tm:16 + tm, :] = xn_ref[...] * (j != tiles_per_seq - 1).astype(F32)
    else:
        p_ref[0:8, :] = jnp.zeros((8, inner), F32)
        p_ref[8 + tm:16 + tm, :] = jnp.zeros((8, inner), F32)
    xc = _silu(p_ref[7:7 + tm, :] * cw_ref[0:1, :] + p_ref[8:8 + tm, :] * cw_ref[1:2, :]
               + p_ref[9:9 + tm, :] * cw_ref[2:3, :] + cb_ref[...])
    if with_xc:
        xc_ref[...] = xc
    xcb = xc.astype(BF16)
    xmb = xm.astype(BF16)
    gates = jnp.zeros((tm, gate_ref.shape[1]), F32) + gb_ref[...]
    for j in range(inner // 128):
        cs = slice(128 * j, 128 * (j + 1))
        qj = _dot(xcb[:, cs], wq_ref[j])
        kj = _dot(xcb[:, cs], wk_ref[j])
        vj = _dot(xmb[:, cs], wv_ref[j])
        qb, kb, vb = qj.astype(BF16), kj.astype(BF16), vj.astype(BF16)
        gates = gates + (_dot(qb, gw_ref[cs, :]) + _dot(kb, gw_ref[inner + 128 * j:inner + 128 * (j + 1), :])
                         + _dot(vb, gw_ref[2 * inner + 128 * j:2 * inner + 128 * (j + 1), :]))
        q_ref[:, cs] = qb
        k_ref[:, cs] = (kj * k_scale).astype(BF16)
        v_ref[:, cs] = vb
    gate_ref[...] = gates


def _expand_blockdiag(w):
    nb, bs, _ = w.shape
    per = 128 // bs
    w4 = w.reshape(nb // per, per, bs, bs)
    eye = jnp.eye(per, dtype=w.dtype)
    return jnp.einsum('jnio,nm->jnimo', w4, eye).reshape(nb // per, 128, 128).astype(BF16)


def _ml_qkv(xm, conv_w, conv_b, w_q, w_k, w_v, gate_w, gate_b, head_dim, seq, with_xc):
    n, inner = xm.shape
    tm = min(seq, 512)
    tps = seq // tm
    halo = tps > 1
    n8 = n // 8
    in_specs = [_rows(tm, inner)]
    args = [xm]
    if halo:
        in_specs += [pl.BlockSpec((8, inner), lambda i: (jnp.maximum(i * (tm // 8) - 1, 0), 0)),
                     pl.BlockSpec((8, inner), lambda i: (jnp.minimum((i + 1) * (tm // 8), n8 - 1), 0))]
        args += [xm, xm]
    nt = inner // 128
    gcols = gate_w.shape[1]
    in_specs += [_resident((3, inner)), _resident((1, inner)), _resident((nt, 128, 128)), _resident((nt, 128, 128)),
                 _resident((nt, 128, 128)), _resident((3 * inner, gcols)), _resident((1, gcols))]
    args += [conv_w, conv_b.reshape(1, inner), _expand_blockdiag(w_q), _expand_blockdiag(w_k),
             _expand_blockdiag(w_v), gate_w, gate_b]
    out_shape = [jax.ShapeDtypeStruct((n, inner), BF16)] * 3 + [jax.ShapeDtypeStruct((n, gcols), F32)]
    out_specs = [_rows(tm, inner)] * 3 + [_rows(tm, gcols)]
    if with_xc:
        out_shape = [jax.ShapeDtypeStruct((n, inner), F32)] + out_shape
        out_specs = [_rows(tm, inner)] + out_specs
    return pl.pallas_call(
        functools.partial(_ml_qkv_kernel, halo=halo, tiles_per_seq=tps, inner=inner,
                          k_scale=float(head_dim) ** -0.5, with_xc=with_xc),
        out_shape=tuple(out_shape),
        grid=(n // tm,),
        in_specs=in_specs,
        out_specs=tuple(out_specs),
        scratch_shapes=[pltpu.VMEM((tm + 16, inner), F32)],
        compiler_params=_cparams(1),
        name="mlstm_qkv",
    )(*args)


def _ml_chunk(rev, i_row, f_row, qb, kb, vb, c_ref, n_ref, m_ref, with_out):
    t = kb.shape[0]
    r = lax.broadcasted_iota(I32, (t, t), 0)
    s = lax.broadcasted_iota(I32, (t, t), 1)
    order = jnp.where(rev, r - s, s - r)
    before = order <= 0
    before_t = order >= 0
    eye = r == s
    lf_row = jax.nn.log_sigmoid(f_row)
    lf_mat = jnp.broadcast_to(lf_row, (t, t))
    b_col = jnp.sum(jnp.where(before, lf_mat, 0.0), axis=1, keepdims=True)
    lf_col = jnp.sum(jnp.where(eye, lf_mat, 0.0), axis=1, keepdims=True)
    i_col = jnp.sum(jnp.where(eye, jnp.broadcast_to(i_row, (t, t)), 0.0), axis=1, keepdims=True)
    b_row = jnp.sum(jnp.where(before_t, jnp.broadcast_to(lf_col, (t, t)), 0.0), axis=0, keepdims=True)
    b_tot = jnp.sum(lf_row, axis=1, keepdims=True)
    m_old = m_ref[0:1, 0:1]
    c_old = c_ref[...]
    n_old = n_ref[...]
    w_log_row = b_tot - b_row + i_row
    m_new = jnp.maximum(b_tot + m_old, jnp.max(w_log_row, axis=1, keepdims=True))
    w_in_col = jnp.exp(b_tot - b_col + i_col - m_new)
    dec = jnp.exp(b_tot + m_old - m_new)
    kw = kb.astype(F32) * w_in_col
    c_ref[...] = dec * c_old + lax.dot_general(kw.astype(BF16), vb, (((0,), (0,)), ((), ())),
                                              preferred_element_type=F32)
    n_ref[...] = dec * n_old + jnp.sum(kw, axis=0, keepdims=True)
    m_ref[...] = jnp.broadcast_to(m_new, m_ref.shape)
    if not with_out:
        return None
    log_d = jnp.where(before, b_col - b_row + i_row, -jnp.inf)
    inter = b_col + m_old
    m_row = jnp.maximum(inter, jnp.max(log_d, axis=1, keepdims=True))
    scores = lax.dot_general(qb, kb, (((1,), (1,)), ((), ())), preferred_element_type=F32)
    s_mat = scores * jnp.exp(log_d - m_row)
    w_prev = jnp.exp(inter - m_row)
    num = w_prev * _dot(qb, c_old.astype(BF16)) + _dot(s_mat.astype(BF16), vb)
    den = (w_prev * jnp.sum(qb.astype(F32) * n_old, axis=1, keepdims=True)
           + jnp.sum(s_mat, axis=1, keepdims=True))
    return num / jnp.maximum(jnp.abs(den), jnp.exp(-m_row))


def _ml_scan_kernel(q_ref, k_ref, v_ref, kc_ref, vc_ref, g_ref, o_ref, c_ref, n_ref, m_ref, *, t):
    rev = pl.program_id(2) == 1
    nc_lat = q_ref.shape[0] // t
    nc_ctx = kc_ref.shape[0] // t
    c_ref[...] = jnp.zeros_like(c_ref)
    n_ref[...] = jnp.zeros_like(n_ref)
    m_ref[...] = jnp.zeros_like(m_ref)

    def ctx_body(c, carry):
        ci = jnp.where(rev, nc_ctx - 1 - c, c)
        rows = pl.ds(pl.multiple_of(ci * t, t), t)
        _ml_chunk(rev, g_ref[0, 0, 0, 0, pl.ds(ci, 1), :], g_ref[0, 0, 0, 1, pl.ds(ci, 1), :],
                  None, kc_ref[rows, :], vc_ref[rows, :], c_ref, n_ref, m_ref, False)
        return carry

    def lat_body(c, carry):
        ci = jnp.where(rev, nc_lat - 1 - c, c)
        rows = pl.ds(pl.multiple_of(ci * t, t), t)
        gi = nc_ctx + ci
        o_ref[0, rows, :] = _ml_chunk(rev, g_ref[0, 0, 0, 0, pl.ds(gi, 1), :], g_ref[0, 0, 0, 1, pl.ds(gi, 1), :],
                                      q_ref[rows, :], k_ref[rows, :], v_ref[rows, :], c_ref, n_ref, m_ref, True)
        return carry

    lax.fori_loop(0, nc_ctx, ctx_body, 0)
    lax.fori_loop(0, nc_lat, lat_body, 0)


def _ml_scan(q, k, v, kc, vc, gates, bsz, seq, n_ctx, heads):
    inner = q.shape[1]
    dh = inner // heads
    t = ML_CHUNK
    nct = (seq + n_ctx) // t
    lat = pl.BlockSpec((seq, dh), lambda b, h, r: (b, h))
    cx = pl.BlockSpec((n_ctx, dh), lambda b, h, r: (b, h))
    return pl.pallas_call(
        functools.partial(_ml_scan_kernel, t=t),
        out_shape=jax.ShapeDtypeStruct((2, bsz * seq, inner), F32),
        grid=(bsz, heads, 2),
        in_specs=[lat, lat, lat, cx, cx,
                  pl.BlockSpec((1, 1, 1, 2, nct, t), lambda b, h, r: (b, h, r, 0, 0, 0))],
        out_specs=pl.BlockSpec((1, seq, dh), lambda b, h, r: (r, b, h)),
        scratch_shapes=[pltpu.VMEM((dh, dh), F32), pltpu.VMEM((1, dh), F32), pltpu.VMEM((8, 128), F32)],
        compiler_params=_cparams(3),
        name="mlstm_scan",
    )(q, k, v, kc, vc, gates)


def _ml_finish_kernel(hf_ref, hb_ref, xc_ref, zs_ref, x_ref, mod_ref, og_ref, sk_ref, w_ref, o_ref, *, d, heads):
    hs = hf_ref[0] + hb_ref[0]
    inner = hs.shape[1]
    dh = inner // heads
    y = None
    for hd in range(heads):
        cs = slice(hd * dh, (hd + 1) * dh)
        seg = hs[:, cs]
        mu = jnp.mean(seg, axis=1, keepdims=True)
        cen = seg - mu
        var = jnp.mean(cen * cen, axis=1, keepdims=True)
        hn = cen * lax.rsqrt(var + EPS) * og_ref[:, cs]
        part = ((hn + sk_ref[:, cs] * xc_ref[:, cs]) * zs_ref[:, cs]).astype(BF16)
        term = _dot(part, w_ref[cs, :])
        y = term if y is None else y + term
    o_ref[...] = x_ref[...] + mod_ref[0, 0][:, 2 * d:3 * d] * y


def _ml_finish(hdir, xc, zs, x, mods, layer, onorm_g, skip, w_down, heads, seq):
    n, d = x.shape
    inner = xc.shape[1]
    tm = min(seq, 512)
    return pl.pallas_call(
        functools.partial(_ml_finish_kernel, d=d, heads=heads),
        out_shape=jax.ShapeDtypeStruct((n, d), F32),
        grid=(n // tm,),
        in_specs=[pl.BlockSpec((1, tm, inner), lambda i: (0, i, 0)), pl.BlockSpec((1, tm, inner), lambda i: (1, i, 0)),
                  _rows(tm, inner), _rows(tm, inner), _rows(tm, d), _mod_spec(layer, 6 * d, seq // tm, None),
                  _resident((1, inner)), _resident((1, inner)), _resident((inner, d))],
        out_specs=_rows(tm, d),
        compiler_params=_cparams(1),
        name="mlstm_finish",
    )(hdir, hdir, xc, zs, x, mods, onorm_g.reshape(1, inner), skip.reshape(1, inner), w_down.astype(BF16))


def _mlstm_layer(x, ctx, mods, layer, norm_g, mp, bsz, seq, n_ctx):
    (w_up, conv_w, conv_b, w_q, w_k, w_v, ig_w, ig_b, fg_w, fg_b, onorm_g, skip, w_down) = mp
    d = x.shape[1]
    inner = w_up.shape[1] // 2
    heads = ig_w.shape[-1]
    dh = inner // heads
    rows = seq // GRID_W
    x_cm = x.reshape(bsz, rows, GRID_W, d).transpose(0, 2, 1, 3).reshape(bsz * seq, d)
    gw = jnp.concatenate([ig_w[0], ig_w[1], fg_w[0], fg_w[1]], axis=1)
    gw = jnp.pad(gw, ((0, 0), (0, 128 - 4 * heads))).astype(BF16)
    gb = jnp.pad(jnp.concatenate([ig_b[0], ig_b[1], fg_b[0], fg_b[1]]), (0, 128 - 4 * heads)).reshape(1, 128)
    xm, zs = _ml_up(x_cm, mods, layer, norm_g, w_up, seq, None, True)
    (xm_c,) = _ml_up(ctx, mods, layer, norm_g, w_up, n_ctx, bsz, False)
    xc, q, k, v, gl = _ml_qkv(xm, conv_w, conv_b, w_q, w_k, w_v, gw, gb, dh, seq, True)
    _, kc, vc, gc = _ml_qkv(xm_c, conv_w, conv_b, w_q, w_k, w_v, gw, gb, dh, n_ctx, False)
    t = ML_CHUNK
    g_all = jnp.concatenate([gc.reshape(bsz, n_ctx, 128), gl.reshape(bsz, seq, 128)], axis=1)[:, :, :4 * heads]
    g_all = g_all.reshape(bsz, (seq + n_ctx) // t, t, 2, 2, heads).transpose(0, 5, 4, 3, 1, 2)
    hdir = _ml_scan(q, k, v, kc, vc, g_all, bsz, seq, n_ctx, heads)
    x_new_cm = _ml_finish(hdir, xc, zs, x_cm, mods, layer, onorm_g, skip, w_down, heads, seq)
    return x_new_cm.reshape(bsz, GRID_W, rows, d).transpose(0, 2, 1, 3).reshape(bsz * seq, d)


def _route_kernel(x_ref, mod_ref, g_ref, rw_ref, rb_ref, h_ref, idx_ref, gate_ref, *, d):
    m = mod_ref[0, 0]
    h2 = _norm_mod(x_ref[...], g_ref[...], m[:, 3 * d:4 * d], m[:, 4 * d:5 * d])
    h_ref[...] = h2
    work = _dot3(h2, rw_ref[...]) + rb_ref[...]
    lane = lax.broadcasted_iota(I32, work.shape, 1).astype(F32)
    vals, ids = [], []
    for _ in range(TOP_K):
        mx = jnp.max(work, axis=1, keepdims=True)
        sel = jnp.min(jnp.where(work == mx, lane, 1e9), axis=1, keepdims=True)
        vals.append(mx)
        ids.append(sel)
        work = jnp.where(lane == sel, -jnp.inf, work)
    ex = [jnp.exp(v - vals[0]) for v in vals]
    inv = 1.0 / (ex[0] + ex[1] + ex[2] + ex[3])
    idx_out = jnp.zeros(work.shape, F32)
    gate_out = jnp.zeros(work.shape, F32)
    for kk in range(TOP_K):
        idx_out = jnp.where(lane == kk, ids[kk], idx_out)
        gate_out = jnp.where(lane == kk, ex[kk] * inv, gate_out)
    idx_ref[...] = idx_out.astype(I32)
    gate_ref[...] = gate_out


def _moe_route(x, mods, layer, norm_g, router_w, router_b, seq, ctx):
    n, d = x.shape
    n_exp = router_w.shape[1]
    tm = min(seq, 512)
    rw = jnp.pad(router_w, ((0, 0), (0, 128 - n_exp)))
    rb = jnp.pad(router_b, (0, 128 - n_exp), constant_values=-1e30).reshape(1, 128)
    return pl.pallas_call(
        functools.partial(_route_kernel, d=d),
        out_shape=(jax.ShapeDtypeStruct((n, d), F32), jax.ShapeDtypeStruct((n, 128), I32),
                   jax.ShapeDtypeStruct((n, 128), F32)),
        grid=(n // tm,),
        in_specs=[_rows(tm, d), _mod_spec(layer, 6 * d, seq // tm, ctx), _resident((1, d)),
                  _resident((d, 128)), _resident((1, 128))],
        out_specs=(_rows(tm, d), _rows(tm, 128), _rows(tm, 128)),
        compiler_params=_cparams(1),
        name="moe_route",
    )(x, mods, norm_g.reshape(1, d), rw, rb)


def _moe_plan(top_idx, n_exp, bm):
    n = top_idx.shape[0]
    npairs = n * TOP_K
    flat_e = top_idx.reshape(-1)
    order = jnp.argsort(flat_e, stable=True).astype(I32)
    sorted_e = flat_e[order]
    counts = jnp.bincount(flat_e, length=n_exp).astype(I32)
    padded = (counts + bm - 1) // bm * bm
    pad_end = jnp.cumsum(padded)
    rank = jnp.arange(npairs, dtype=I32) - (jnp.cumsum(counts) - counts)[sorted_e]
    dest = (pad_end - padded)[sorted_e] + rank
    n_blocks = -(-npairs // bm) + n_exp
    nslots = n_blocks * bm
    slot_pair = jnp.full((nslots,), -1, I32).at[dest].set(order)
    valid = slot_pair >= 0
    pair = jnp.where(valid, slot_pair, npairs + jnp.arange(nslots, dtype=I32) % (2 * bm))
    tok = jnp.where(valid, slot_pair // TOP_K, 0)
    idx = jnp.stack([tok, pair // TOP_K, pair % TOP_K]).reshape(3, n_blocks, bm).transpose(1, 0, 2)
    block_e = jnp.minimum(jnp.searchsorted(pad_end, jnp.arange(n_blocks, dtype=I32) * bm, side='right'),
                          n_exp - 1).astype(I32)
    return idx, block_e


def _moe_kernel(be_ref, cur_ref, nxt_ref, h_hbm, wg_ref, wu_ref, wd_ref, bg_ref, bu_ref, bd_ref, y_hbm,
                xbuf, ybuf, scur, snxt, gsem, ssem, isem, *, bm, nb, d):
    b = pl.program_id(0)
    slot = b % 2
    nslot = 1 - slot
    ic = pltpu.make_async_copy(cur_ref.at[0], scur, isem.at[0])
    inx = pltpu.make_async_copy(nxt_ref.at[0], snxt, isem.at[1])
    ic.start()
    inx.start()
    ic.wait()
    inx.wait()

    def gather(idx_smem, to_slot):
        def body(r, carry):
            tok = idx_smem[0, r]
            pltpu.make_async_copy(h_hbm.at[pl.ds(tok, 1)], xbuf.at[to_slot, pl.ds(r, 1)], gsem.at[to_slot]).start()
            return carry
        lax.fori_loop(0, bm, body, 0, unroll=8)

    @pl.when(b == 0)
    def _():
        gather(scur, 0)

    @pl.when(b + 1 < nb)
    def _():
        gather(snxt, nslot)

    pltpu.make_async_copy(h_hbm.at[pl.ds(0, bm)], xbuf.at[slot], gsem.at[slot]).wait()

    def scatter_wait(s):
        pltpu.make_async_copy(ybuf.at[s], y_hbm.at[pl.ds(0, bm), pl.ds(0, d)], ssem.at[s]).wait()

    @pl.when(b >= 2)
    def _():
        scatter_wait(slot)

    x = xbuf[slot].astype(BF16)
    f = wg_ref.shape[2]
    fc = min(f, 512)
    acc = None
    for c0 in range(0, f, fc):
        cs = slice(c0, c0 + fc)
        g = _dot(x, wg_ref[0, :, cs]) + bg_ref[0, :, cs]
        u = _dot(x, wu_ref[0, :, cs]) + bu_ref[0, :, cs]
        gt = jnp.minimum(g, SWIGLU_LIMIT)
        up = jnp.clip(u, -SWIGLU_LIMIT, SWIGLU_LIMIT)
        act = ((up + 1.0) * gt * jax.nn.sigmoid(SWIGLU_ALPHA * gt)).astype(BF16)
        term = _dot(act, wd_ref[0, cs, :])
        acc = term if acc is None else acc + term
    ybuf[slot] = acc + bd_ref[0]

    def scatter_body(r, carry):
        row = scur[1, r]
        col = pl.multiple_of(scur[2, r] * d, 128)
        pltpu.make_async_copy(ybuf.at[slot, pl.ds(r, 1)], y_hbm.at[pl.ds(row, 1), pl.ds(col, d)],
                              ssem.at[slot]).start()
        return carry
    lax.fori_loop(0, bm, scatter_body, 0, unroll=8)

    @pl.when(b == nb - 1)
    def _():
        scatter_wait(slot)
        if nb >= 2:
            scatter_wait(nslot)


def _moe_experts(h2, idx, block_e, w_gu, b_gu, w_down, b_down, bm):
    n, d = h2.shape
    n_exp, _, f2 = w_gu.shape
    f = f2 // 2
    nb = idx.shape[0]
    wg = w_gu[:, :, 0::2].astype(BF16)
    wu = w_gu[:, :, 1::2].astype(BF16)
    wd = w_down.astype(BF16)
    bg = b_gu[:, 0::2].reshape(n_exp, 1, f)
    bu = b_gu[:, 1::2].reshape(n_exp, 1, f)
    bd = b_down.reshape(n_exp, 1, d)
    grid_spec = pltpu.PrefetchScalarGridSpec(
        num_scalar_prefetch=1,
        grid=(nb,),
        in_specs=[pl.BlockSpec((1, 3, bm), lambda b, be: (b, 0, 0)),
                  pl.BlockSpec((1, 3, bm), lambda b, be: (jnp.minimum(b + 1, nb - 1), 0, 0)),
                  pl.BlockSpec(memory_space=pl.ANY),
                  pl.BlockSpec((1, d, f), lambda b, be: (be[b], 0, 0)),
                  pl.BlockSpec((1, d, f), lambda b, be: (be[b], 0, 0)),
                  pl.BlockSpec((1, f, d), lambda b, be: (be[b], 0, 0)),
                  pl.BlockSpec((1, 1, f), lambda b, be: (be[b], 0, 0)),
                  pl.BlockSpec((1, 1, f), lambda b, be: (be[b], 0, 0)),
                  pl.BlockSpec((1, 1, d), lambda b, be: (be[b], 0, 0))],
        out_specs=pl.BlockSpec(memory_space=pl.ANY),
        scratch_shapes=[pltpu.VMEM((2, bm, d), F32), pltpu.VMEM((2, bm, d), F32),
                        pltpu.SMEM((3, bm), I32), pltpu.SMEM((3, bm), I32),
                        pltpu.SemaphoreType.DMA((2,)), pltpu.SemaphoreType.DMA((2,)),
                        pltpu.SemaphoreType.DMA((2,))],
    )
    return pl.pallas_call(
        functools.partial(_moe_kernel, bm=bm, nb=nb, d=d),
        out_shape=jax.ShapeDtypeStruct((n + 2 * bm // TOP_K, TOP_K * d), F32),
        grid_spec=grid_spec,
        compiler_params=_cparams(1),
        name="moe_experts",
    )(block_e, idx, idx, h2, wg, wu, wd, bg, bu, bd)


def _combine_kernel(*refs, d, final):
    if final:
        y_ref, g_ref, x_ref, mod_ref, fg_ref, o_ref = refs
    else:
        y_ref, g_ref, x_ref, mod_ref, o_ref = refs
    g = g_ref[...]
    acc = g[:, 0:1] * y_ref[:, 0:d]
    for kk in range(1, TOP_K):
        acc = acc + g[:, kk:kk + 1] * y_ref[:, kk * d:(kk + 1) * d]
    out = x_ref[...] + mod_ref[0, 0][:, 5 * d:6 * d] * acc
    if final:
        ms = jnp.mean(out * out, axis=-1, keepdims=True)
        out = out * lax.rsqrt(ms + EPS) * fg_ref[...]
    o_ref[...] = out


def _moe_combine(y, gates, x, mods, layer, seq, ctx, row_off, final_g):
    n, d = x.shape
    tm = min(seq, 256)
    off = row_off // tm
    final = final_g is not None
    in_specs = [_rows(tm, TOP_K * d, off), _rows(tm, 128, off), _rows(tm, d),
                _mod_spec(layer, 6 * d, seq // tm, ctx)]
    args = [y, gates, x, mods]
    if final:
        in_specs.append(_resident((1, d)))
        args.append(final_g.reshape(1, d))
    return pl.pallas_call(
        functools.partial(_combine_kernel, d=d, final=final),
        out_shape=jax.ShapeDtypeStruct((n, d), F32),
        grid=(n // tm,),
        in_specs=in_specs,
        out_specs=_rows(tm, d),
        compiler_params=_cparams(1),
        name="moe_combine",
    )(*args)


def _moe_layer(x, ctx, mods, layer, norm_g, mp, bsz, seq, n_ctx, final_g):
    (router_w, router_b, w_gu, b_gu, w_down, b_down) = mp
    n_exp = router_w.shape[1]
    h2, idx, gate = _moe_route(x, mods, layer, norm_g, router_w, router_b, seq, None)
    n_c = 0
    if ctx is not None:
        h2c, idxc, gatec = _moe_route(ctx, mods, layer, norm_g, router_w, router_b, n_ctx, bsz)
        n_c = ctx.shape[0]
        h2 = jnp.concatenate([h2c, h2], axis=0)
        idx = jnp.concatenate([idxc, idx], axis=0)
        gate = jnp.concatenate([gatec, gate], axis=0)
    slots, block_e = _moe_plan(idx[:, :TOP_K], n_exp, MOE_BM)
    y = _moe_experts(h2, slots, block_e, w_gu, b_gu, w_down, b_down, MOE_BM)
    x_new = _moe_combine(y, gate, x, mods, layer, seq, None, n_c, final_g)
    ctx_new = None
    if ctx is not None:
        ctx_new = _moe_combine(y, gate, ctx, mods, layer, n_ctx, bsz, 0, None)
    return x_new, ctx_new


def kernel(x, c, ctx, c_ctx, ada_w, ada_b, norm1_g, norm2_g, router_w, router_b, moe_w_gu, moe_b_gu, moe_w_down, moe_b_down, hy_w_in, hy_b_in, hy_conv_w, hy_conv_b, hy_f_w1, hy_f_b1, hy_f_w2, hy_f_b2, hy_f_w3, hy_f_freq, hy_f_bias, hy_w_out, hy_b_out, s5_a_re, s5_a_im, s5_log_step, s5_b_re, s5_b_im, s5_c_re, s5_c_im, s5_d, s5_glu_w1, s5_glu_b1, s5_glu_w2, s5_glu_b2, ml_w_up, ml_conv_w, ml_conv_b, ml_w_q, ml_w_k, ml_w_v, ml_ig_w, ml_ig_b, ml_fg_w, ml_fg_b, ml_onorm_g, ml_skip, ml_w_down, final_norm_g):
    bsz, seq, d = x.shape
    n_ctx = ctx.shape[1]
    depth = ada_w.shape[0]
    rows = -(-(bsz + 1) // 8) * 8
    c_all = jnp.zeros((rows, d), F32).at[:bsz].set(c).at[bsz].set(c_ctx)
    mods = _adaln(c_all, ada_w, ada_b)
    xs = x.reshape(bsz * seq, d)
    cs = ctx.reshape(bsz * n_ctx, d)
    for i in range(depth):
        kind, j = i % 3, i // 3
        ctx_live = any(l % 3 != 0 for l in range(i + 1, depth))
        if kind == 0:
            hp = (hy_w_in[j], hy_b_in[j], hy_conv_w[j], hy_conv_b[j], hy_f_w1[j], hy_f_b1[j], hy_f_w2[j],
                  hy_f_b2[j], hy_f_w3[j], hy_f_freq[j], hy_f_bias[j], hy_w_out[j], hy_b_out[j])
            if ctx_live:
                cs = _hyena_layer(cs, mods, i, norm1_g[i], hp, n_ctx, bsz)
            xs = _hyena_layer(xs, mods, i, norm1_g[i], hp, seq, None)
        elif kind == 1:
            sp = (s5_a_re[j], s5_a_im[j], s5_log_step[j], s5_b_re[j], s5_b_im[j], s5_c_re[j], s5_c_im[j],
                  s5_d[j], s5_glu_w1[j], s5_glu_b1[j], s5_glu_w2[j], s5_glu_b2[j])
            xs, cs_new = _s5_layer(xs, cs, mods, i, norm1_g[i], sp, bsz, seq, n_ctx)
            if ctx_live:
                cs = cs_new
        else:
            assert not ctx_live, "context outputs of the mLSTM mixer are not implemented"
            mp = (ml_w_up[j], ml_conv_w[j], ml_conv_b[j], ml_w_q[j], ml_w_k[j], ml_w_v[j], ml_ig_w[j], ml_ig_b[j],
                  ml_fg_w[j], ml_fg_b[j], ml_onorm_g[j], ml_skip[j], ml_w_down[j])
            xs = _mlstm_layer(xs, cs, mods, i, norm1_g[i], mp, bsz, seq, n_ctx)
        ep = (router_w[i], router_b[i], moe_w_gu[i], moe_b_gu[i], moe_w_down[i], moe_b_down[i])
        xs, cs_new = _moe_layer(xs, cs if ctx_live else None, mods, i, norm2_g[i], ep, bsz, seq, n_ctx,
                                final_norm_g if i == depth - 1 else None)
        if ctx_live:
            cs = cs_new
    return xs.reshape(bsz, seq, d)
```
